```python
import jax, jax.numpy as jnp
from jax import lax
import numpy as np

D_MODEL = 1024
BATCH = 2
SEQ = 16384
DEPTH = 1
DEC_BATCH = 32
DEC_SEQ = 64
PAST_LEN = 4096

CHUNK = 64
N_META = 16
GLA_HEADS = 4
GLA_DK = 128
GLA_DV = 256
QK_WIDTH = GLA_HEADS * GLA_DK
V_WIDTH = GLA_HEADS * GLA_DV
GK_RANK = 16
GATE_NORMALIZER = 16.0
CONV_DIM = D_MODEL
CONV_WIDTH = 31
N_GROUPS = 4
EXPERTS_PER_GROUP = 8
N_EXPERTS = N_GROUPS * EXPERTS_PER_GROUP
TOP_K = 2
D_EXPERT = 512
MOE_BLOCK = 128
IN_COLS = 2 * QK_WIDTH + 2 * V_WIDTH + GK_RANK + 2 * CONV_DIM + 2 * D_MODEL
EPS = 1e-6

kernel_name = "hybrid_gla_conformer_hmoe_stream_step"


def rms_norm(x, g):
    xf = x.astype(jnp.float32)
    y = xf * lax.rsqrt(jnp.mean(xf * xf, axis=-1, keepdims=True) + EPS)
    return (y * g.astype(jnp.float32)).astype(x.dtype)


def layer_norm(x, g, b):
    xf = x.astype(jnp.float32)
    mu = jnp.mean(xf, axis=-1, keepdims=True)
    var = jnp.mean(jnp.square(xf - mu), axis=-1, keepdims=True)
    y = (xf - mu) * lax.rsqrt(var + EPS) * g.astype(jnp.float32) + b.astype(jnp.float32)
    return y.astype(x.dtype)


def split_points():
    sizes = (QK_WIDTH, QK_WIDTH, V_WIDTH, V_WIDTH, GK_RANK, CONV_DIM, CONV_DIM, D_MODEL, D_MODEL)
    pts, acc = [], 0
    for s in sizes[:-1]:
        acc += s
        pts.append(acc)
    return pts


def gla_chunked(q, k, v, log_g, s0, chunk):
    B, L, H, DK = q.shape
    DV = v.shape[-1]
    n = L // chunk

    def to_chunks(t):
        return jnp.moveaxis(t.reshape(B, n, chunk, H, t.shape[-1]), 1, 0)

    causal = jnp.tril(jnp.ones((chunk, chunk), dtype=bool))[None, :, :, None, None]

    def step(s, inp):
        qi, ki, vi, gi = inp
        b = jnp.cumsum(gi.astype(jnp.float32), axis=1)
        o_inter = jnp.einsum('bihk,bhkv->bihv', qi * jnp.exp(b), s)
        diff = b[:, :, None] - b[:, None, :]
        decay = jnp.exp(jnp.where(causal, diff, -jnp.inf))
        scores = jnp.einsum('bihk,bjhk,bijhk->bhij', qi, ki, decay)
        o_intra = jnp.einsum('bhij,bjhv->bihv', scores, vi)
        b_last = b[:, -1]
        k_dec = ki * jnp.exp(b_last[:, None] - b)
        s_new = s * jnp.exp(b_last)[..., None] + jnp.einsum('bjhk,bjhv->bhkv', k_dec, vi)
        return s_new, o_inter + o_intra

    s_fin, o = lax.scan(step, s0.astype(jnp.float32),
                        (to_chunks(q), to_chunks(k), to_chunks(v), to_chunks(log_g)))
    o = jnp.moveaxis(o, 0, 1).reshape(B, L, H, DV)
    return o.astype(v.dtype), s_fin.astype(s0.dtype)


def token_mixer(n, s0, conv_hist, pad_front, chunk, w_in, w_gk_up, b_gk, gla_norm_g, w_gla_out,
                conv_w, conv_b, conv_ln_g, conv_ln_b, w_conv_out, w_out):
    B, L, _ = n.shape
    proj = n @ w_in
    q, k, v, g_out, gk_lr, glu_a, glu_b, gate_a, gate_b = jnp.split(proj, split_points(), axis=-1)

    log_g = jax.nn.log_sigmoid((gk_lr @ w_gk_up + b_gk).astype(jnp.float32)) / GATE_NORMALIZER
    q = q.reshape(B, L, GLA_HEADS, GLA_DK) * (GLA_DK ** -0.5)
    k = k.reshape(B, L, GLA_HEADS, GLA_DK)
    v = v.reshape(B, L, GLA_HEADS, GLA_DV)
    log_g = log_g.reshape(B, L, GLA_HEADS, GLA_DK)
    pad = ((0, 0), (pad_front, 0), (0, 0), (0, 0))
    o, s_new = gla_chunked(jnp.pad(q, pad), jnp.pad(k, pad), jnp.pad(v, pad), jnp.pad(log_g, pad), s0, chunk)
    o = o[:, pad_front:]
    o = rms_norm(o, gla_norm_g) * jax.nn.silu(g_out.reshape(B, L, GLA_HEADS, GLA_DV))
    y_a = o.reshape(B, L, V_WIDTH) @ w_gla_out

    u = glu_a * jax.nn.sigmoid(glu_b)
    u_full = jnp.concatenate([conv_hist.astype(u.dtype), u], axis=1)
    c = lax.conv_general_dilated(u_full, conv_w[:, None, :].astype(u.dtype), window_strides=(1,),
                                 padding='VALID', dimension_numbers=('NWC', 'WIO', 'NWC'),
                                 feature_group_count=CONV_DIM) + conv_b
    c = jax.nn.silu(layer_norm(c, conv_ln_g, conv_ln_b))
    y_b = c @ w_conv_out

    merged = jax.nn.sigmoid(gate_a) * y_a + jax.nn.sigmoid(gate_b) * y_b
    return merged @ w_out, s_new, u_full[:, -(CONV_WIDTH - 1):]


def hier_moe(h, w_router_group, w_router_expert, w_exp_gate, w_exp_up, w_exp_down):
    B, L, D = h.shape
    x = h.reshape(B * L, D)
    T = x.shape[0]
    logit_grp = jnp.einsum('td,dg->tg', x, w_router_group).astype(jnp.float32)
    p_grp = jax.nn.softmax(logit_grp, axis=-1)
    grp = jnp.argmax(logit_grp, axis=-1).astype(jnp.int32)
    logit_exp = jnp.einsum('td,de->te', x, w_router_expert).astype(jnp.float32)
    logit_exp = logit_exp.reshape(T, N_GROUPS, EXPERTS_PER_GROUP)
    logit_in = jnp.take_along_axis(logit_exp, grp[:, None, None], axis=1)[:, 0]
    top_val, top_idx = lax.top_k(logit_in, TOP_K)
    gate = jax.nn.softmax(top_val, axis=-1) * jnp.take_along_axis(p_grp, grp[:, None], axis=1)
    expert = grp[:, None] * EXPERTS_PER_GROUP + top_idx.astype(jnp.int32)

    n_assign = T * TOP_K
    e_flat = expert.reshape(-1)
    tok_flat = jnp.repeat(jnp.arange(T, dtype=jnp.int32), TOP_K)
    w_flat = gate.reshape(-1)
    order = jnp.argsort(e_flat)
    e_sorted, tok_sorted, w_sorted = e_flat[order], tok_flat[order], w_flat[order]
    counts = jnp.zeros((N_EXPERTS,), jnp.int32).at[e_flat].add(1)
    starts = jnp.cumsum(counts) - counts
    padded = (counts + MOE_BLOCK - 1) // MOE_BLOCK * MOE_BLOCK
    pad_ends = jnp.cumsum(padded)
    pad_starts = pad_ends - padded
    dest = pad_starts[e_sorted] + (jnp.arange(n_assign, dtype=jnp.int32) - starts[e_sorted])
    n_blocks = -(-n_assign // MOE_BLOCK) + N_EXPERTS
    buf_tok = jnp.zeros((n_blocks * MOE_BLOCK,), jnp.int32).at[dest].set(tok_sorted)
    buf_w = jnp.zeros((n_blocks * MOE_BLOCK,), jnp.float32).at[dest].set(w_sorted)
    blk_expert = jnp.minimum(
        jnp.searchsorted(pad_ends, jnp.arange(n_blocks, dtype=jnp.int32) * MOE_BLOCK, side='right'),
        N_EXPERTS - 1)

    def expert_block(args):
        tok_b, e = args
        xb = x[tok_b]
        hb = jax.nn.silu(xb @ w_exp_gate[e]) * (xb @ w_exp_up[e])
        return hb @ w_exp_down[e]

    y_blocks = lax.map(expert_block, (buf_tok.reshape(n_blocks, MOE_BLOCK), blk_expert))
    y = jnp.zeros((T, D), jnp.float32).at[buf_tok].add(
        y_blocks.reshape(-1, D).astype(jnp.float32) * buf_w[:, None])
    return y.astype(h.dtype).reshape(B, L, D)


def setup_inputs(seed: int = 0) -> dict:
    key = jax.random.key(seed)
    ks = jax.random.split(key, 26)
    nrm = lambda k, shape, s: jax.random.normal(k, shape, jnp.float32) * s
    return {
        'x_prompt': nrm(ks[0], (BATCH, SEQ, D_MODEL), 1.0),
        'x_sample': nrm(ks[1], (DEC_BATCH, DEC_SEQ, D_MODEL), 1.0),
        'state_gla': nrm(ks[2], (DEPTH, DEC_BATCH, GLA_HEADS, GLA_DK, GLA_DV), 0.5),
        'cache_conv': nrm(ks[3], (DEPTH, DEC_BATCH, CONV_WIDTH - 1, CONV_DIM), 1.0),
        'meta_tokens': nrm(ks[4], (N_META, D_MODEL), 1.0),
        'norm_mix_g': 1.0 + nrm(ks[5], (DEPTH, D_MODEL), 0.02),
        'w_in': nrm(ks[6], (DEPTH, D_MODEL, IN_COLS), D_MODEL ** -0.5),
        'w_gk_up': nrm(ks[7], (DEPTH, GK_RANK, QK_WIDTH), GK_RANK ** -0.5),
        'b_gk': nrm(ks[8], (DEPTH, QK_WIDTH), 0.1),
        'gla_norm_g': 1.0 + nrm(ks[9], (DEPTH, GLA_DV), 0.02),
        'w_gla_out': nrm(ks[10], (DEPTH, V_WIDTH, D_MODEL), V_WIDTH ** -0.5),
        'conv_w': nrm(ks[11], (DEPTH, CONV_WIDTH, CONV_DIM), CONV_WIDTH ** -0.5),
        'conv_b': nrm(ks[12], (DEPTH, CONV_DIM), 0.02),
        'conv_ln_g': 1.0 + nrm(ks[13], (DEPTH, CONV_DIM), 0.02),
        'conv_ln_b': nrm(ks[14], (DEPTH, CONV_DIM), 0.02),
        'w_conv_out': nrm(ks[15], (DEPTH, CONV_DIM, D_MODEL), CONV_DIM ** -0.5),
        'w_out': nrm(ks[16], (DEPTH, D_MODEL, D_MODEL), D_MODEL ** -0.5),
        'norm_ffn_g': 1.0 + nrm(ks[17], (DEPTH, D_MODEL), 0.02),
        'w_router_group': nrm(ks[18], (DEPTH, D_MODEL, N_GROUPS), D_MODEL ** -0.5),
        'w_router_expert': nrm(ks[19], (DEPTH, D_MODEL, N_EXPERTS), D_MODEL ** -0.5),
        'w_exp_gate': nrm(ks[20], (DEPTH, N_EXPERTS, D_MODEL, D_EXPERT), D_MODEL ** -0.5),
        'w_exp_up': nrm(ks[21], (DEPTH, N_EXPERTS, D_MODEL, D_EXPERT), D_MODEL ** -0.5),
        'w_exp_down': nrm(ks[22], (DEPTH, N_EXPERTS, D_EXPERT, D_MODEL), D_EXPERT ** -0.5),
        'norm_final_g': 1.0 + nrm(ks[23], (D_MODEL,), 0.02),
    }


def reference(x_prompt, x_sample, state_gla, cache_conv, meta_tokens, norm_mix_g, w_in, w_gk_up, b_gk,
              gla_norm_g, w_gla_out, conv_w, conv_b, conv_ln_g, conv_ln_b, w_conv_out, w_out, norm_ffn_g,
              w_router_group, w_router_expert, w_exp_gate, w_exp_up, w_exp_down, norm_final_g):
    bp = x_prompt.shape[0]
    meta = jnp.broadcast_to(meta_tokens[None].astype(x_prompt.dtype), (bp, N_META, D_MODEL))
    hp = jnp.concatenate([meta, x_prompt], axis=1)
    hs = x_sample
    pad_front = (-hp.shape[1]) % CHUNK
    zero_state = jnp.zeros((bp, GLA_HEADS, GLA_DK, GLA_DV), jnp.float32)
    zero_conv = jnp.zeros((bp, CONV_WIDTH - 1, CONV_DIM), hp.dtype)
    sp_list, cp_list, ss_list, cs_list = [], [], [], []
    for l in range(DEPTH):
        mix_w = (w_in[l], w_gk_up[l], b_gk[l], gla_norm_g[l], w_gla_out[l], conv_w[l], conv_b[l],
                 conv_ln_g[l], conv_ln_b[l], w_conv_out[l], w_out[l])
        mp, sp, cp = token_mixer(rms_norm(hp, norm_mix_g[l]), zero_state, zero_conv, pad_front, CHUNK, *mix_w)
        ms, ss, cs = token_mixer(rms_norm(hs, norm_mix_g[l]), state_gla[l], cache_conv[l], 0, hs.shape[1], *mix_w)
        hp = hp + mp
        hs = hs + ms
        moe_w = (w_router_group[l], w_router_expert[l], w_exp_gate[l], w_exp_up[l], w_exp_down[l])
        hp = hp + hier_moe(rms_norm(hp, norm_ffn_g[l]), *moe_w)
        hs = hs + hier_moe(rms_norm(hs, norm_ffn_g[l]), *moe_w)
        sp_list.append(sp)
        cp_list.append(cp)
        ss_list.append(ss)
        cs_list.append(cs)
    y_prompt = rms_norm(hp, norm_final_g)[:, N_META:]
    y_sample = rms_norm(hs, norm_final_g)
    return (y_prompt, y_sample, jnp.stack(sp_list), jnp.stack(cp_list), jnp.stack(ss_list), jnp.stack(cs_list))
```

```python
import functools

import jax
import jax.numpy as jnp
from jax import lax
from jax.experimental import pallas as pl
from jax.experimental.pallas import tpu as pltpu

D_MODEL = 1024
N_META = 16
CHUNK = 64
GLA_HEADS = 4
GLA_DK = 128
GLA_DV = 256
QK_WIDTH = GLA_HEADS * GLA_DK
V_WIDTH = GLA_HEADS * GLA_DV
GK_RANK = 16
GATE_NORMALIZER = 16.0
CONV_DIM = D_MODEL
CONV_WIDTH = 31
N_GROUPS = 4
EXPERTS_PER_GROUP = 8
N_EXPERTS = N_GROUPS * EXPERTS_PER_GROUP
TOP_K = 2
D_EXPERT = 512
EPS = 1e-6

LANES = 128
SUBLANES = 8
HIST_ROWS = 32
HIST_SKIP = HIST_ROWS - (CONV_WIDTH - 1)
MIX_ROWS = 512
SAMPLE_SEGS = 4
TOK_BLOCK = 512
EXP_BLOCK = 256
ROUTE_ROWS = 40
VMEM_LIMIT = 56 * 1024 * 1024

_BF = jnp.bfloat16
_F32 = jnp.float32
_NT = (((1,), (1,)), ((), ()))
_TN = (((0,), (0,)), ((), ()))


def _mm(a, b):
    return jnp.dot(a, b, preferred_element_type=_F32)


def _sigmoid(x):
    return 1.0 / (1.0 + jnp.exp(-x))


def _log_sigmoid(x):
    return jnp.minimum(x, 0.0) - jnp.log1p(jnp.exp(-jnp.abs(x)))


def _mixer_kernel(x_ref, sin_ref, hin_ref, gmix_ref, w1_ref, wgk_ref, wgu_ref, bgk_ref, gng_ref, wgo_ref,
                  cw_ref, cb_ref, lng_ref, lnb_ref, wco_ref, w2_ref, wout_ref,
                  h_ref, sout_ref, hout_ref,
                  q_scr, k_scr, v_scr, lg_scr, sg_scr, og_scr, uext_scr, cc_scr, st_scr,
                  *, nseg, seg, carry, n_front_pad, n_inner):
    i = pl.program_id(1)
    rows = nseg * seg
    n_chunks = rows // CHUNK

    x = x_ref[...].reshape(rows, D_MODEL)
    ms = jnp.mean(x * x, axis=-1, keepdims=True)
    n = (x * lax.rsqrt(ms + EPS) * gmix_ref[...]).astype(_BF)

    q_scr[...] = _mm(n, w1_ref[:, 0:QK_WIDTH]) * (GLA_DK ** -0.5)
    k_scr[...] = _mm(n, w1_ref[:, QK_WIDTH:2 * QK_WIDTH])
    v_scr[...] = _mm(n, w1_ref[:, 2 * QK_WIDTH:2 * QK_WIDTH + V_WIDTH]).astype(_BF)
    g_out = _mm(n, w1_ref[:, 2 * QK_WIDTH + V_WIDTH:2 * QK_WIDTH + 2 * V_WIDTH])
    sg_scr[...] = (g_out * _sigmoid(g_out)).astype(_BF)
    gk_lr = _mm(n, wgk_ref[...])
    gk = _mm(gk_lr.astype(_BF), wgu_ref[...]) + bgk_ref[...]
    lg = _log_sigmoid(gk) * (1.0 / GATE_NORMALIZER)
    if n_front_pad:
        row_id = lax.broadcasted_iota(jnp.int32, (rows, QK_WIDTH), 0)
        lg = jnp.where(row_id >= n_front_pad, lg, 0.0)
    lg_scr[...] = lg

    glu_a = _mm(n, w2_ref[:, 0:CONV_DIM])
    glu_b = _mm(n, w2_ref[:, CONV_DIM:2 * CONV_DIM])
    u = glu_a * _sigmoid(glu_b)

    if carry:
        @pl.when(i == 0)
        def _():
            uext_scr[:, 0:HIST_ROWS, :] = hin_ref[...]
    else:
        uext_scr[:, 0:HIST_ROWS, :] = hin_ref[...]
    uext_scr[:, HIST_ROWS:HIST_ROWS + seg, :] = u.reshape(nseg, seg, CONV_DIM)
    hout_ref[...] = uext_scr[:, seg:seg + HIST_ROWS, :]

    if carry:
        @pl.when(i == 0)
        def _():
            for hd in range(GLA_HEADS):
                st_scr[hd] = sin_ref[0, hd].T

    r_io = lax.broadcasted_iota(jnp.int32, (CHUNK, CHUNK), 0)
    c_io = lax.broadcasted_iota(jnp.int32, (CHUNK, CHUNK), 1)
    causal = r_io >= c_io
    tri = causal.astype(_BF)
    gng = gng_ref[...]

    def chunk_body(c, carry_val):
        r0 = pl.multiple_of(c * CHUNK, CHUNK)
        rs = pl.ds(r0, CHUNK)
        if not carry:
            for hd in range(GLA_HEADS):
                st_scr[hd] = sin_ref[c, hd].T
        lgc = lg_scr[rs, :]
        p0 = lgc.astype(_BF)
        r1 = lgc - p0.astype(_F32)
        p1 = r1.astype(_BF)
        p2 = (r1 - p1.astype(_F32)).astype(_BF)
        b = _mm(tri, p0) + _mm(tri, p1) + _mm(tri, p2)
        b_last = b[CHUNK - 1:CHUNK, :]
        b_mid = b[CHUNK // 2 - 1:CHUNK // 2, :]
        e_b = jnp.exp(b)
        e_q = jnp.exp(b - b_mid)
        e_k = jnp.exp(b_mid - b)
        e_l = jnp.exp(b_last - b)
        e_last = jnp.exp(b_last)
        for hd in range(GLA_HEADS):
            ks = slice(hd * GLA_DK, (hd + 1) * GLA_DK)
            vs = slice(hd * GLA_DV, (hd + 1) * GLA_DV)
            qh = q_scr[rs, ks]
            kh = k_scr[rs, ks]
            vh = v_scr[rs, vs]
            st = st_scr[hd]
            o_inter = lax.dot_general((qh * e_b[:, ks]).astype(_BF), st.astype(_BF), _NT,
                                      preferred_element_type=_F32)
            a = lax.dot_general((qh * e_q[:, ks]).astype(_BF), (kh * e_k[:, ks]).astype(_BF), _NT,
                                preferred_element_type=_F32)
            a = jnp.where(causal, a, 0.0)
            o = o_inter + _mm(a.astype(_BF), vh)
            kd = (kh * e_l[:, ks]).astype(_BF)
            st_scr[hd] = st * e_last[:, ks] + lax.dot_general(vh, kd, _TN, preferred_element_type=_F32)
            oms = jnp.mean(o * o, axis=-1, keepdims=True)
            on = o * lax.rsqrt(oms + EPS) * gng
            og_scr[rs, vs] = (on * sg_scr[rs, vs].astype(_F32)).astype(_BF)
        if not carry:
            for hd in range(GLA_HEADS):
                sout_ref[c, hd] = st_scr[hd].T
        return carry_val

    lax.fori_loop(0, n_chunks, chunk_body, 0)

    if carry:
        @pl.when(i == n_inner - 1)
        def _():
            for hd in range(GLA_HEADS):
                sout_ref[0, hd] = st_scr[hd].T

    y_a = _mm(og_scr[...], wgo_ref[...])

    row_tile = 32
    lane_tile = 256
    for s in range(nseg):
        for rt in range(seg // row_tile):
            base = rt * row_tile
            for lt in range(CONV_DIM // lane_tile):
                ls = slice(lt * lane_tile, (lt + 1) * lane_tile)
                n_grp = row_tile // SUBLANES
                acc = [jnp.broadcast_to(cb_ref[:, ls], (SUBLANES, lane_tile)) for _ in range(n_grp)]
                for sft in range(SUBLANES):
                    a8s = [a8 for a8 in range(-(-(CONV_WIDTH + HIST_SKIP) // SUBLANES))
                           if 0 <= sft - HIST_SKIP + SUBLANES * a8 < CONV_WIDTH]
                    win = uext_scr[s, pl.ds(base + sft, SUBLANES * (a8s[-1] + n_grp)), ls]
                    for a8 in a8s:
                        w_t = cw_ref[sft - HIST_SKIP + SUBLANES * a8, :, ls]
                        for rr in range(n_grp):
                            lo_row = SUBLANES * (a8 + rr)
                            acc[rr] = acc[rr] + win[lo_row:lo_row + SUBLANES, :] * w_t
                for rr in range(n_grp):
                    cc_scr[pl.ds(s * seg + base + SUBLANES * rr, SUBLANES), ls] = acc[rr]

    if carry:
        uext_scr[:, 0:HIST_ROWS, :] = uext_scr[:, seg:seg + HIST_ROWS, :]

    cc = cc_scr[...]
    mu = jnp.mean(cc, axis=-1, keepdims=True)
    cen = cc - mu
    var = jnp.mean(cen * cen, axis=-1, keepdims=True)
    cn = cen * lax.rsqrt(var + EPS) * lng_ref[...] + lnb_ref[...]
    cact = (cn * _sigmoid(cn)).astype(_BF)
    y_b = _mm(cact, wco_ref[...])

    gate_a = _sigmoid(_mm(n, w2_ref[:, 2 * CONV_DIM:2 * CONV_DIM + D_MODEL]))
    gate_b = _sigmoid(_mm(n, w2_ref[:, 2 * CONV_DIM + D_MODEL:2 * CONV_DIM + 2 * D_MODEL]))
    merged = (gate_a * y_a + gate_b * y_b).astype(_BF)
    out = _mm(merged, wout_ref[...])
    h_ref[...] = (x_ref[...].reshape(rows, D_MODEL) + out).reshape(nseg, seg, D_MODEL)


def _mixer_call(x, s_in, h_in, weights, *, nseg, seg, carry, n_front_pad, shared_init):
    n_streams, length, _ = x.shape
    rows = nseg * seg
    if carry:
        assert nseg == 1 and length % seg == 0
        grid = (n_streams, length // seg)
        x_map = lambda o, i: (o, i, 0)
        so_map = lambda o, i: (o, 0, 0, 0)
        ho_map = lambda o, i: (o, 0, 0)
    else:
        assert seg == length == CHUNK and n_streams % nseg == 0
        grid = (n_streams // nseg, 1)
        x_map = lambda o, i: (o, 0, 0)
        so_map = lambda o, i: (o, 0, 0, 0)
        ho_map = lambda o, i: (o, 0, 0)
    if shared_init:
        si_map = lambda o, i: (0, 0, 0, 0)
        hi_map = lambda o, i: (0, 0, 0)
    else:
        si_map, hi_map = so_map, ho_map
    n_inner = grid[1]
    whole = pl.BlockSpec(memory_space=pltpu.VMEM)
    kern = functools.partial(_mixer_kernel, nseg=nseg, seg=seg, carry=carry, n_front_pad=n_front_pad,
                             n_inner=n_inner)
    return pl.pallas_call(
        kern,
        grid=grid,
        in_specs=[
            pl.BlockSpec((nseg, seg, D_MODEL), x_map),
            pl.BlockSpec((nseg, GLA_HEADS, GLA_DK, GLA_DV), si_map),
            pl.BlockSpec((nseg, HIST_ROWS, CONV_DIM), hi_map),
        ] + [whole] * len(weights),
        out_specs=[
            pl.BlockSpec((nseg, seg, D_MODEL), x_map),
            pl.BlockSpec((nseg, GLA_HEADS, GLA_DK, GLA_DV), so_map),
            pl.BlockSpec((nseg, HIST_ROWS, CONV_DIM), ho_map),
        ],
        out_shape=[
            jax.ShapeDtypeStruct((n_streams, length, D_MODEL), _F32),
            jax.ShapeDtypeStruct((n_streams, GLA_HEADS, GLA_DK, GLA_DV), _F32),
            jax.ShapeDtypeStruct((n_streams, HIST_ROWS, CONV_DIM), _F32),
        ],
        scratch_shapes=[
            pltpu.VMEM((rows, QK_WIDTH), _F32),
            pltpu.VMEM((rows, QK_WIDTH), _F32),
            pltpu.VMEM((rows, V_WIDTH), _BF),
            pltpu.VMEM((rows, QK_WIDTH), _F32),
            pltpu.VMEM((rows, V_WIDTH), _BF),
            pltpu.VMEM((rows, V_WIDTH), _BF),
            pltpu.VMEM((nseg, HIST_ROWS + seg, CONV_DIM), _F32),
            pltpu.VMEM((rows, CONV_DIM), _F32),
            pltpu.VMEM((GLA_HEADS, GLA_DV, GLA_DK), _F32),
        ],
        compiler_params=pltpu.CompilerParams(
            dimension_semantics=("arbitrary", "arbitrary"), vmem_limit_bytes=VMEM_LIMIT),
        name="mixer_carry" if carry else "mixer_segs",
    )(x, s_in, h_in, *weights)


def _route_kernel(h_ref, g_ref, wr_ref, run0_ref, xn_ref, ri_ref, gcol_ref, cnt_ref, run_scr):
    i = pl.program_id(0)
    tb = h_ref.shape[0]

    @pl.when(i == 0)
    def _():
        run_scr[...] = run0_ref[...]

    h = h_ref[...]
    ms = jnp.mean(h * h, axis=-1, keepdims=True)
    xn = h * lax.rsqrt(ms + EPS) * g_ref[...]

    xb = xn.astype(_BF)
    half = D_MODEL // 2
    hi = lax.bitcast_convert_type(xb[:, :half].astype(_F32), jnp.uint32) & jnp.uint32(0xFFFF0000)
    lo = lax.bitcast_convert_type(xb[:, half:].astype(_F32), jnp.uint32) >> 16
    xn_ref[...] = hi | lo

    logit = lax.dot_general(wr_ref[...], xn, _NT, precision=lax.Precision.HIGHEST,
                            preferred_element_type=_F32)
    lgrp = logit[0:N_GROUPS, :]
    gmax = jnp.max(lgrp, axis=0, keepdims=True)
    io4 = lax.broadcasted_iota(jnp.int32, (N_GROUPS, tb), 0)
    grp = jnp.min(jnp.where(lgrp == gmax, io4, N_GROUPS), axis=0, keepdims=True)
    p_sel = 1.0 / jnp.sum(jnp.exp(lgrp - gmax), axis=0, keepdims=True)

    lin = jnp.zeros((EXPERTS_PER_GROUP, tb), _F32)
    for g in range(N_GROUPS):
        lin = jnp.where(grp == g, logit[8 + 8 * g:16 + 8 * g, :], lin)
    io8 = lax.broadcasted_iota(jnp.int32, (EXPERTS_PER_GROUP, tb), 0)
    t1 = jnp.max(lin, axis=0, keepdims=True)
    i1 = jnp.min(jnp.where(lin == t1, io8, EXPERTS_PER_GROUP), axis=0, keepdims=True)
    lin2 = jnp.where(io8 == i1, -jnp.inf, lin)
    t2 = jnp.max(lin2, axis=0, keepdims=True)
    i2 = jnp.min(jnp.where(lin2 == t2, io8, EXPERTS_PER_GROUP), axis=0, keepdims=True)
    e21 = jnp.exp(t2 - t1)
    den = 1.0 + e21
    g1 = (1.0 / den) * p_sel
    g2 = (e21 / den) * p_sel
    e1 = grp * EXPERTS_PER_GROUP + i1
    e2 = grp * EXPERTS_PER_GROUP + i2

    io32 = lax.broadcasted_iota(jnp.int32, (N_EXPERTS, tb), 0)
    m1 = io32 == e1
    m2 = io32 == e2
    mh = jnp.logical_or(m1, m2).astype(_F32)
    upper = (lax.broadcasted_iota(jnp.int32, (tb, tb), 0) < lax.broadcasted_iota(jnp.int32, (tb, tb), 1))
    cum = _mm(mh.astype(_BF), upper.astype(_BF))
    pos = cum + run_scr[...]
    rank1 = jnp.sum(jnp.where(m1, pos, 0.0), axis=0, keepdims=True).astype(jnp.int32)
    rank2 = jnp.sum(jnp.where(m2, pos, 0.0), axis=0, keepdims=True).astype(jnp.int32)
    run_new = run_scr[...] + jnp.sum(mh, axis=1, keepdims=True)
    run_scr[...] = run_new
    cnt_ref[...] = run_new

    io_r = lax.broadcasted_iota(jnp.int32, (SUBLANES, tb), 0)
    ri = jnp.where(io_r == 0, e1, jnp.where(io_r == 1, e2, jnp.where(io_r == 2, rank1,
                                                                      jnp.where(io_r == 3, rank2, 0))))
    ri_ref[0] = ri
    io_g = lax.broadcasted_iota(jnp.int32, (LANES, tb), 0)
    gmat = jnp.where(io_g == 0, g1, jnp.where(io_g == 1, g2, 0.0))
    gcol_ref[...] = gmat.T


def _route_call(h2d, g_ffn, wr_t, run0):
    t = h2d.shape[0]
    nblk = t // TOK_BLOCK
    return pl.pallas_call(
        _route_kernel,
        grid=(nblk,),
        in_specs=[
            pl.BlockSpec((TOK_BLOCK, D_MODEL), lambda i: (i, 0)),
            pl.BlockSpec((1, D_MODEL), lambda i: (0, 0)),
            pl.BlockSpec((ROUTE_ROWS, D_MODEL), lambda i: (0, 0)),
            pl.BlockSpec((N_EXPERTS, TOK_BLOCK), lambda i: (0, 0)),
        ],
        out_specs=[
            pl.BlockSpec((TOK_BLOCK, D_MODEL // 2), lambda i: (i, 0)),
            pl.BlockSpec((1, SUBLANES, TOK_BLOCK), lambda i: (i, 0, 0)),
            pl.BlockSpec((TOK_BLOCK, LANES), lambda i: (i, 0)),
            pl.BlockSpec((N_EXPERTS, TOK_BLOCK), lambda i: (0, 0)),
        ],
        out_shape=[
            jax.ShapeDtypeStruct((t, D_MODEL // 2), jnp.uint32),
            jax.ShapeDtypeStruct((nblk, SUBLANES, TOK_BLOCK), jnp.int32),
            jax.ShapeDtypeStruct((t, LANES), _F32),
            jax.ShapeDtypeStruct((N_EXPERTS, TOK_BLOCK), _F32),
        ],
        scratch_shapes=[pltpu.VMEM((N_EXPERTS, TOK_BLOCK), _F32)],
        compiler_params=pltpu.CompilerParams(dimension_semantics=("arbitrary",), vmem_limit_bytes=VMEM_LIMIT),
        name="route",
    )(h2d, g_ffn, wr_t, run0)


def _row_copy(src_ref, src_row, dst_ref, dst_row, sem):
    return pltpu.make_async_copy(src_ref.at[pl.ds(src_row, 1)], dst_ref.at[pl.ds(dst_row, 1)], sem)


def _dispatch_kernel(dest_ref, src_ref, xs_in_ref, xs_ref, sem):
    del xs_in_ref
    i = pl.program_id(0)
    tb = src_ref.shape[0]
    base = i * (TOP_K * tb)

    def issue(r, cv):
        for k in range(TOP_K):
            _row_copy(src_ref, r, xs_ref, dest_ref[base + k * tb + r], sem).start()
        return cv

    lax.fori_loop(0, tb, issue, 0)

    def drain(r, cv):
        for k in range(TOP_K):
            _row_copy(src_ref, r, xs_ref, dest_ref[base + k * tb + r], sem).wait()
        return cv

    lax.fori_loop(0, tb, drain, 0)


def _dispatch_call(dest_flat, xn_packed, xs):
    t = xn_packed.shape[0]
    nblk = t // TOK_BLOCK
    return pl.pallas_call(
        _dispatch_kernel,
        grid_spec=pltpu.PrefetchScalarGridSpec(
            num_scalar_prefetch=1,
            grid=(nblk,),
            in_specs=[
                pl.BlockSpec((TOK_BLOCK, D_MODEL // 2), lambda i, d: (i, 0)),
                pl.BlockSpec(memory_space=pl.ANY),
            ],
            out_specs=pl.BlockSpec(memory_space=pl.ANY),
            scratch_shapes=[pltpu.SemaphoreType.DMA(())],
        ),
        out_shape=jax.ShapeDtypeStruct(xs.shape, xs.dtype),
        input_output_aliases={2: 0},
        compiler_params=pltpu.CompilerParams(dimension_semantics=("arbitrary",)),
        name="dispatch",
    )(dest_flat, xn_packed, xs)


def _expert_kernel(be_ref, bi_ref, nu_ref, xs_ref, wg_ref, wu_ref, wd_ref, ys_ref, wg_scr, wu_scr, wd_scr):
    i = pl.program_id(0)
    prev = be_ref[jnp.maximum(i - 1, 0)]
    fresh = jnp.logical_or(i == 0, be_ref[i] != prev)

    @pl.when(fresh)
    def _():
        wg_scr[...] = wg_ref[0].astype(_BF)
        wu_scr[...] = wu_ref[0].astype(_BF)
        wd_scr[...] = wd_ref[0].astype(_BF)

    @pl.when(i < nu_ref[0])
    def _():
        half = D_MODEL // 2
        packed = xs_ref[...]
        x_hi = lax.bitcast_convert_type(packed & jnp.uint32(0xFFFF0000), _F32).astype(_BF)
        x_lo = lax.bitcast_convert_type(packed << 16, _F32).astype(_BF)
        hg = _mm(x_hi, wg_scr[0:half, :]) + _mm(x_lo, wg_scr[half:D_MODEL, :])
        hu = _mm(x_hi, wu_scr[0:half, :]) + _mm(x_lo, wu_scr[half:D_MODEL, :])
        hb = (hg * _sigmoid(hg) * hu).astype(_BF)
        ys_ref[...] = _mm(hb, wd_scr[...])

    @pl.when(i >= nu_ref[0])
    def _():
        ys_ref[...] = jnp.zeros_like(ys_ref)


def _expert_call(blk_expert, blk_index, n_used, xs, w_gate, w_up, w_down):
    n_slots = xs.shape[0]
    n_blocks = n_slots // EXP_BLOCK
    return pl.pallas_call(
        _expert_kernel,
        grid_spec=pltpu.PrefetchScalarGridSpec(
            num_scalar_prefetch=3,
            grid=(n_blocks,),
            in_specs=[
                pl.BlockSpec((EXP_BLOCK, D_MODEL // 2), lambda i, be, bi, nu: (bi[i], 0)),
                pl.BlockSpec((1, D_MODEL, D_EXPERT), lambda i, be, bi, nu: (be[i], 0, 0)),
                pl.BlockSpec((1, D_MODEL, D_EXPERT), lambda i, be, bi, nu: (be[i], 0, 0)),
                pl.BlockSpec((1, D_EXPERT, D_MODEL), lambda i, be, bi, nu: (be[i], 0, 0)),
            ],
            out_specs=pl.BlockSpec((EXP_BLOCK, D_MODEL), lambda i, be, bi, nu: (i, 0)),
            scratch_shapes=[
                pltpu.VMEM((D_MODEL, D_EXPERT), _BF),
                pltpu.VMEM((D_MODEL, D_EXPERT), _BF),
                pltpu.VMEM((D_EXPERT, D_MODEL), _BF),
            ],
        ),
        out_shape=jax.ShapeDtypeStruct((n_slots, D_MODEL), _F32),
        compiler_params=pltpu.CompilerParams(dimension_semantics=("arbitrary",), vmem_limit_bytes=VMEM_LIMIT),
        name="experts",
    )(blk_expert, blk_index, n_used, xs, w_gate, w_up, w_down)


def _combine_kernel(dest_ref, h_ref, gcol_ref, gfin_ref, ys_ref, out_ref, y1_scr, y2_scr, sem):
    i = pl.program_id(0)
    tb = h_ref.shape[0]
    base = i * (TOP_K * tb)
    bufs = (y1_scr, y2_scr)

    def issue(r, cv):
        for k in range(TOP_K):
            _row_copy(ys_ref, dest_ref[base + k * tb + r], bufs[k], r, sem).start()
        return cv

    lax.fori_loop(0, tb, issue, 0)

    def drain(r, cv):
        for k in range(TOP_K):
            _row_copy(ys_ref, dest_ref[base + k * tb + r], bufs[k], r, sem).wait()
        return cv

    lax.fori_loop(0, tb, drain, 0)

    gates = gcol_ref[...]
    moe = gates[:, 0:1] * y1_scr[...] + gates[:, 1:2] * y2_scr[...]
    y = h_ref[...] + moe
    ms = jnp.mean(y * y, axis=-1, keepdims=True)
    out_ref[...] = y * lax.rsqrt(ms + EPS) * gfin_ref[...]


def _combine_call(dest_flat, h2d, gcol, g_final, ys):
    t = h2d.shape[0]
    nblk = t // TOK_BLOCK
    return pl.pallas_call(
        _combine_kernel,
        grid_spec=pltpu.PrefetchScalarGridSpec(
            num_scalar_prefetch=1,
            grid=(nblk,),
            in_specs=[
                pl.BlockSpec((TOK_BLOCK, D_MODEL), lambda i, d: (i, 0)),
                pl.BlockSpec((TOK_BLOCK, LANES), lambda i, d: (i, 0)),
                pl.BlockSpec((1, D_MODEL), lambda i, d: (0, 0)),
                pl.BlockSpec(memory_space=pl.ANY),
            ],
            out_specs=pl.BlockSpec((TOK_BLOCK, D_MODEL), lambda i, d: (i, 0)),
            scratch_shapes=[
                pltpu.VMEM((TOK_BLOCK, D_MODEL), _F32),
                pltpu.VMEM((TOK_BLOCK, D_MODEL), _F32),
                pltpu.SemaphoreType.DMA(()),
            ],
        ),
        out_shape=jax.ShapeDtypeStruct((t, D_MODEL), _F32),
        compiler_params=pltpu.CompilerParams(dimension_semantics=("arbitrary",), vmem_limit_bytes=VMEM_LIMIT),
        name="combine",
    )(dest_flat, h2d, gcol, g_final, ys)


def _dest_flat(ri, pad_starts):
    dest = pad_starts[ri[:, 0:TOP_K, :]] + ri[:, TOP_K:2 * TOP_K, :]
    return dest.reshape(-1)


def kernel(x_prompt, x_sample, state_gla, cache_conv, meta_tokens, norm_mix_g, w_in, w_gk_up, b_gk, gla_norm_g, w_gla_out, conv_w, conv_b, conv_ln_g, conv_ln_b, w_conv_out, w_out, norm_ffn_g, w_router_group, w_router_expert, w_exp_gate, w_exp_up, w_exp_down, norm_final_g):
    depth = w_in.shape[0]
    assert depth == 1
    bp, seq, _ = x_prompt.shape
    bs, dec_seq, _ = x_sample.shape
    pad_front = (-(N_META + seq)) % CHUNK
    assert pad_front + N_META == CHUNK and seq % MIX_ROWS == 0 and dec_seq == CHUNK

    l = 0
    c_gk = 2 * QK_WIDTH + 2 * V_WIDTH
    w1 = w_in[l][:, :c_gk].astype(_BF)
    wgk = jnp.pad(w_in[l][:, c_gk:c_gk + GK_RANK], ((0, 0), (0, LANES - GK_RANK))).astype(_BF)
    w2 = w_in[l][:, c_gk + GK_RANK:].astype(_BF)
    wgu = jnp.pad(w_gk_up[l], ((0, LANES - GK_RANK), (0, 0))).astype(_BF)
    cw8 = jnp.broadcast_to(conv_w[l][:, None, :], (CONV_WIDTH, SUBLANES, CONV_DIM))
    row = lambda v: v.reshape(1, -1)
    weights = (row(norm_mix_g[l]), w1, wgk, wgu, row(b_gk[l]), row(gla_norm_g[l]), w_gla_out[l].astype(_BF),
               cw8, row(conv_b[l]), row(conv_ln_g[l]), row(conv_ln_b[l]), w_conv_out[l].astype(_BF), w2,
               w_out[l].astype(_BF))

    x_meta = jnp.concatenate([jnp.zeros((pad_front, D_MODEL), _F32), meta_tokens.astype(_F32)], axis=0)[None]
    zero_state = jnp.zeros((1, GLA_HEADS, GLA_DK, GLA_DV), _F32)
    zero_hist = jnp.zeros((1, HIST_ROWS, CONV_DIM), _F32)
    _, s_meta, h_meta = _mixer_call(x_meta, zero_state, zero_hist, weights, nseg=1, seg=CHUNK, carry=True,
                                    n_front_pad=pad_front, shared_init=True)
    h_p, s_p, hist_p = _mixer_call(x_prompt, s_meta, h_meta, weights, nseg=1, seg=MIX_ROWS, carry=True,
                                   n_front_pad=0, shared_init=True)
    hist_s_in = jnp.pad(cache_conv[l], ((0, 0), (HIST_SKIP, 0), (0, 0)))
    h_s, s_s, hist_s = _mixer_call(x_sample, state_gla[l], hist_s_in, weights, nseg=SAMPLE_SEGS, seg=CHUNK,
                                   carry=False, n_front_pad=0, shared_init=False)

    wr_t = jnp.concatenate([w_router_group[l].T, jnp.zeros((SUBLANES - N_GROUPS, D_MODEL), _F32),
                            w_router_expert[l].T], axis=0)
    g_ffn = row(norm_ffn_g[l])
    hp2 = h_p.reshape(bp * seq, D_MODEL)
    hs2 = h_s.reshape(bs * dec_seq, D_MODEL)
    run0 = jnp.zeros((N_EXPERTS, TOK_BLOCK), _F32)
    xn_p, ri_p, gc_p, cnt_p = _route_call(hp2, g_ffn, wr_t, run0)
    xn_s, ri_s, gc_s, cnt_s = _route_call(hs2, g_ffn, wr_t, cnt_p)

    counts = cnt_s[:, 0].astype(jnp.int32)
    padded = (counts + EXP_BLOCK - 1) // EXP_BLOCK * EXP_BLOCK
    pad_ends = jnp.cumsum(padded)
    pad_starts = pad_ends - padded
    n_assign = TOP_K * (bp * seq + bs * dec_seq)
    n_blocks = -(-n_assign // EXP_BLOCK) + N_EXPERTS
    n_used = pad_ends[-1] // EXP_BLOCK
    blk_ids = jnp.minimum(jnp.arange(n_blocks, dtype=jnp.int32), n_used - 1)
    blk_expert = jnp.minimum(jnp.searchsorted(pad_ends, blk_ids * EXP_BLOCK, side='right'),
                             N_EXPERTS - 1).astype(jnp.int32)
    dest_p = _dest_flat(ri_p, pad_starts)
    dest_s = _dest_flat(ri_s, pad_starts)

    xs = jnp.zeros((n_blocks * EXP_BLOCK, D_MODEL // 2), jnp.uint32)
    xs = _dispatch_call(dest_p, xn_p, xs)
    xs = _dispatch_call(dest_s, xn_s, xs)
    ys = _expert_call(blk_expert, blk_ids, n_used.reshape(1).astype(jnp.int32), xs,
                      w_exp_gate[l], w_exp_up[l], w_exp_down[l])
    g_fin = row(norm_final_g)
    y_p = _combine_call(dest_p, hp2, gc_p, g_fin, ys)
    y_s = _combine_call(dest_s, hs2, gc_s, g_fin, ys)

    return (y_p.reshape(bp, seq, D_MODEL), y_s.reshape(bs, dec_seq, D_MODEL),
            s_p[None], hist_p[None, :, HIST_SKIP:, :], s_s[None], hist_s[None, :, HIST_SKIP:, :])
```

```python
import functools

import jax
import jax.numpy as jnp
from jax import lax
from jax.experimental import pallas as pl
from jax.experimental.pallas import tpu as pltpu

D_MODEL = 1024
N_META = 16
CHUNK = 64
GLA_HEADS = 4
GLA_DK = 128
GLA_DV = 256
QK_WIDTH = GLA_HEADS * GLA_DK
V_WIDTH = GLA_HEADS * GLA_DV
GK_RANK = 16
GATE_NORMALIZER = 16.0
CONV_DIM = D_MODEL
CONV_WIDTH = 31
N_GROUPS = 4
EXPERTS_PER_GROUP = 8
N_EXPERTS = N_GROUPS * EXPERTS_PER_GROUP
TOP_K = 2
D_EXPERT = 512
EPS = 1e-6

LANES = 128
SUBLANES = 8
HIST_ROWS = 32
HIST_SKIP = HIST_ROWS - (CONV_WIDTH - 1)
MIX_ROWS = 512
SAMPLE_SEGS = 4
TOK_BLOCK = 512
EXP_BLOCK = 256
ROUTE_ROWS = 40
DMA_UNROLL = 8
VMEM_LIMIT = 56 * 1024 * 1024

_BF = jnp.bfloat16
_F32 = jnp.float32
_NT = (((1,), (1,)), ((), ()))
_TN = (((0,), (0,)), ((), ()))


def _mm(a, b):
    return jnp.dot(a, b, preferred_element_type=_F32)


def _sigmoid(x):
    return 1.0 / (1.0 + jnp.exp(-x))


def _log_sigmoid(x):
    return jnp.minimum(x, 0.0) - jnp.log1p(jnp.exp(-jnp.abs(x)))


def _conv_pitch(seg):
    p = -(-seg // SUBLANES)
    while p % 8 != 4:
        p += 1
    return p


def _mixer_kernel(x_ref, sin_ref, hin_ref, gmix_ref, w1_ref, wgk_ref, wgu_ref, bgk_ref, gng_ref, wgo_ref,
                  cw_ref, cb_ref, lng_ref, lnb_ref, wco_ref, w2_ref, wout_ref,
                  h_ref, sout_ref, hout_ref,
                  q_scr, k_scr, v_scr, lg_scr, sg_scr, og_scr, uext_scr, cc_scr, st_scr,
                  *, nseg, seg, carry, n_front_pad, n_inner):
    i = pl.program_id(1)
    rows = nseg * seg
    n_chunks = rows // CHUNK

    x = x_ref[...].reshape(rows, D_MODEL)
    ms = jnp.mean(x * x, axis=-1, keepdims=True)
    n = (x * lax.rsqrt(ms + EPS) * gmix_ref[...]).astype(_BF)

    q_scr[...] = _mm(n, w1_ref[:, 0:QK_WIDTH]) * (GLA_DK ** -0.5)
    k_scr[...] = _mm(n, w1_ref[:, QK_WIDTH:2 * QK_WIDTH])
    v_scr[...] = _mm(n, w1_ref[:, 2 * QK_WIDTH:2 * QK_WIDTH + V_WIDTH]).astype(_BF)
    g_out = _mm(n, w1_ref[:, 2 * QK_WIDTH + V_WIDTH:2 * QK_WIDTH + 2 * V_WIDTH])
    sg_scr[...] = (g_out * _sigmoid(g_out)).astype(_BF)
    gk_lr = _mm(n, wgk_ref[...])
    gk = _mm(gk_lr.astype(_BF), wgu_ref[...]) + bgk_ref[...]
    lg = _log_sigmoid(gk) * (1.0 / GATE_NORMALIZER)
    if n_front_pad:
        row_id = lax.broadcasted_iota(jnp.int32, (rows, QK_WIDTH), 0)
        lg = jnp.where(row_id >= n_front_pad, lg, 0.0)
    lg_scr[...] = lg

    glu_a = _mm(n, w2_ref[:, 0:CONV_DIM])
    glu_b = _mm(n, w2_ref[:, CONV_DIM:2 * CONV_DIM])
    u = glu_a * _sigmoid(glu_b)

    pitch = _conv_pitch(seg)
    n_slab = CONV_DIM // LANES
    live_rows = HIST_ROWS + seg

    @pl.when(jnp.logical_and(pl.program_id(0) == 0, i == 0))
    def _():
        uext_scr[:, :, live_rows:, :] = jnp.zeros(
            (nseg, n_slab, uext_scr.shape[2] - live_rows, LANES), _F32)

    def load_history():
        for sl in range(n_slab):
            uext_scr[:, sl, 0:HIST_ROWS, :] = hin_ref[:, :, sl * LANES:(sl + 1) * LANES]

    if carry:
        pl.when(i == 0)(load_history)
    else:
        load_history()
    u3 = u.reshape(nseg, seg, CONV_DIM)
    for sl in range(n_slab):
        uext_scr[:, sl, HIST_ROWS:live_rows, :] = u3[:, :, sl * LANES:(sl + 1) * LANES]
        hout_ref[:, :, sl * LANES:(sl + 1) * LANES] = uext_scr[:, sl, seg:live_rows, :]

    if carry:
        @pl.when(i == 0)
        def _():
            for hd in range(GLA_HEADS):
                st_scr[hd] = sin_ref[0, hd].T

    r_io = lax.broadcasted_iota(jnp.int32, (CHUNK, CHUNK), 0)
    c_io = lax.broadcasted_iota(jnp.int32, (CHUNK, CHUNK), 1)
    causal = r_io >= c_io
    tri = causal.astype(_BF)
    gng = gng_ref[...]

    def chunk_body(c, carry_val):
        r0 = pl.multiple_of(c * CHUNK, CHUNK)
        rs = pl.ds(r0, CHUNK)
        if not carry:
            for hd in range(GLA_HEADS):
                st_scr[hd] = sin_ref[c, hd].T
        lgc = lg_scr[rs, :]
        p0 = lgc.astype(_BF)
        r1 = lgc - p0.astype(_F32)
        p1 = r1.astype(_BF)
        p2 = (r1 - p1.astype(_F32)).astype(_BF)
        b = _mm(tri, p0) + _mm(tri, p1) + _mm(tri, p2)
        b_last = b[CHUNK - 1:CHUNK, :]
        b_mid = b[CHUNK // 2 - 1:CHUNK // 2, :]
        e_b = jnp.exp(b)
        e_q = jnp.exp(b - b_mid)
        e_k = jnp.exp(b_mid - b)
        e_l = jnp.exp(b_last - b)
        e_last = jnp.exp(b_last)
        for hd in range(GLA_HEADS):
            ks = slice(hd * GLA_DK, (hd + 1) * GLA_DK)
            vs = slice(hd * GLA_DV, (hd + 1) * GLA_DV)
            qh = q_scr[rs, ks]
            kh = k_scr[rs, ks]
            vh = v_scr[rs, vs]
            st = st_scr[hd]
            o_inter = lax.dot_general((qh * e_b[:, ks]).astype(_BF), st.astype(_BF), _NT,
                                      preferred_element_type=_F32)
            a = lax.dot_general((qh * e_q[:, ks]).astype(_BF), (kh * e_k[:, ks]).astype(_BF), _NT,
                                preferred_element_type=_F32)
            a = jnp.where(causal, a, 0.0)
            o = o_inter + _mm(a.astype(_BF), vh)
            kd = (kh * e_l[:, ks]).astype(_BF)
            st_scr[hd] = st * e_last[:, ks] + lax.dot_general(vh, kd, _TN, preferred_element_type=_F32)
            oms = jnp.mean(o * o, axis=-1, keepdims=True)
            on = o * lax.rsqrt(oms + EPS) * gng
            og_scr[rs, vs] = (on * sg_scr[rs, vs].astype(_F32)).astype(_BF)
        if not carry:
            for hd in range(GLA_HEADS):
                sout_ref[c, hd] = st_scr[hd].T
        return carry_val

    lax.fori_loop(0, n_chunks, chunk_body, 0)

    if carry:
        @pl.when(i == n_inner - 1)
        def _():
            for hd in range(GLA_HEADS):
                sout_ref[0, hd] = st_scr[hd].T

    y_a = _mm(og_scr[...], wgo_ref[...])

    group = 4
    for s in range(nseg):
        for sl in range(n_slab):
            lsl = slice(sl * LANES, (sl + 1) * LANES)
            taps = [cw_ref[t, :, lsl] for t in range(CONV_WIDTH)]
            bias = jnp.broadcast_to(cb_ref[:, lsl], (SUBLANES, LANES))
            for j0 in range(0, pitch, group):
                n_g = min(group, pitch - j0)
                acc = [bias] * n_g
                for off in range(j0, j0 + n_g + CONV_WIDTH - 1):
                    uv = uext_scr[s, sl, pl.ds(HIST_SKIP + off, SUBLANES, stride=pitch), :]
                    for g in range(n_g):
                        t = off - (j0 + g)
                        if 0 <= t < CONV_WIDTH:
                            acc[g] = acc[g] + uv * taps[t]
                for g in range(n_g):
                    cc_scr[s, sl, pl.ds(j0 + g, SUBLANES, stride=pitch), :] = acc[g]

    if carry:
        for sl in range(n_slab):
            uext_scr[:, sl, 0:HIST_ROWS, :] = uext_scr[:, sl, seg:live_rows, :]

    cc = jnp.concatenate(
        [jnp.concatenate([cc_scr[s, sl, 0:seg, :] for sl in range(n_slab)], axis=1) for s in range(nseg)],
        axis=0)
    mu = jnp.mean(cc, axis=-1, keepdims=True)
    cen = cc - mu
    var = jnp.mean(cen * cen, axis=-1, keepdims=True)
    cn = cen * lax.rsqrt(var + EPS) * lng_ref[...] + lnb_ref[...]
    cact = (cn * _sigmoid(cn)).astype(_BF)
    y_b = _mm(cact, wco_ref[...])

    gate_a = _sigmoid(_mm(n, w2_ref[:, 2 * CONV_DIM:2 * CONV_DIM + D_MODEL]))
    gate_b = _sigmoid(_mm(n, w2_ref[:, 2 * CONV_DIM + D_MODEL:2 * CONV_DIM + 2 * D_MODEL]))
    merged = (gate_a * y_a + gate_b * y_b).astype(_BF)
    out = _mm(merged, wout_ref[...])
    h_ref[...] = (x_ref[...].reshape(rows, D_MODEL) + out).reshape(nseg, seg, D_MODEL)


def _mixer_call(x, s_in, h_in, weights, *, nseg, seg, carry, n_front_pad, shared_init):
    n_streams, length, _ = x.shape
    rows = nseg * seg
    if carry:
        assert nseg == 1 and length % seg == 0
        grid = (n_streams, length // seg)
        x_map = lambda o, i: (o, i, 0)
        so_map = lambda o, i: (o, 0, 0, 0)
        ho_map = lambda o, i: (o, 0, 0)
    else:
        assert seg == length == CHUNK and n_streams % nseg == 0
        grid = (n_streams // nseg, 1)
        x_map = lambda o, i: (o, 0, 0)
        so_map = lambda o, i: (o, 0, 0, 0)
        ho_map = lambda o, i: (o, 0, 0)
    if shared_init:
        si_map = lambda o, i: (0, 0, 0, 0)
        hi_map = lambda o, i: (0, 0, 0)
    else:
        si_map, hi_map = so_map, ho_map
    n_inner = grid[1]
    pitch = _conv_pitch(seg)
    whole = pl.BlockSpec(memory_space=pltpu.VMEM)
    kern = functools.partial(_mixer_kernel, nseg=nseg, seg=seg, carry=carry, n_front_pad=n_front_pad,
                             n_inner=n_inner)
    return pl.pallas_call(
        kern,
        grid=grid,
        in_specs=[
            pl.BlockSpec((nseg, seg, D_MODEL), x_map),
            pl.BlockSpec((nseg, GLA_HEADS, GLA_DK, GLA_DV), si_map),
            pl.BlockSpec((nseg, HIST_ROWS, CONV_DIM), hi_map),
        ] + [whole] * len(weights),
        out_specs=[
            pl.BlockSpec((nseg, seg, D_MODEL), x_map),
            pl.BlockSpec((nseg, GLA_HEADS, GLA_DK, GLA_DV), so_map),
            pl.BlockSpec((nseg, HIST_ROWS, CONV_DIM), ho_map),
        ],
        out_shape=[
            jax.ShapeDtypeStruct((n_streams, length, D_MODEL), _F32),
            jax.ShapeDtypeStruct((n_streams, GLA_HEADS, GLA_DK, GLA_DV), _F32),
            jax.ShapeDtypeStruct((n_streams, HIST_ROWS, CONV_DIM), _F32),
        ],
        scratch_shapes=[
            pltpu.VMEM((rows, QK_WIDTH), _F32),
            pltpu.VMEM((rows, QK_WIDTH), _F32),
            pltpu.VMEM((rows, V_WIDTH), _BF),
            pltpu.VMEM((rows, QK_WIDTH), _F32),
            pltpu.VMEM((rows, V_WIDTH), _BF),
            pltpu.VMEM((rows, V_WIDTH), _BF),
            pltpu.VMEM((nseg, CONV_DIM // LANES, SUBLANES * pitch + HIST_ROWS, LANES), _F32),
            pltpu.VMEM((nseg, CONV_DIM // LANES, SUBLANES * pitch, LANES), _F32),
            pltpu.VMEM((GLA_HEADS, GLA_DV, GLA_DK), _F32),
        ],
        compiler_params=pltpu.CompilerParams(
            dimension_semantics=("arbitrary", "arbitrary"), vmem_limit_bytes=VMEM_LIMIT),
        name="mixer_carry" if carry else "mixer_segs",
    )(x, s_in, h_in, *weights)


def _route_kernel(h_ref, g_ref, wr_ref, run0_ref, xn_ref, ri_ref, gcol_ref, cnt_ref, run_scr):
    i = pl.program_id(0)
    tb = h_ref.shape[0]

    @pl.when(i == 0)
    def _():
        run_scr[...] = run0_ref[...]

    h = h_ref[...]
    ms = jnp.mean(h * h, axis=-1, keepdims=True)
    xn = h * lax.rsqrt(ms + EPS) * g_ref[...]

    xb = xn.astype(_BF)
    half = D_MODEL // 2
    hi = lax.bitcast_convert_type(xb[:, :half].astype(_F32), jnp.uint32) & jnp.uint32(0xFFFF0000)
    lo = lax.bitcast_convert_type(xb[:, half:].astype(_F32), jnp.uint32) >> 16
    xn_ref[...] = hi | lo

    logit = lax.dot_general(wr_ref[...], xn, _NT, precision=lax.Precision.HIGHEST,
                            preferred_element_type=_F32)
    lgrp = logit[0:N_GROUPS, :]
    gmax = jnp.max(lgrp, axis=0, keepdims=True)
    io4 = lax.broadcasted_iota(jnp.int32, (N_GROUPS, tb), 0)
    grp = jnp.min(jnp.where(lgrp == gmax, io4, N_GROUPS), axis=0, keepdims=True)
    p_sel = 1.0 / jnp.sum(jnp.exp(lgrp - gmax), axis=0, keepdims=True)

    lin = jnp.zeros((EXPERTS_PER_GROUP, tb), _F32)
    for g in range(N_GROUPS):
        lin = jnp.where(grp == g, logit[8 + 8 * g:16 + 8 * g, :], lin)
    io8 = lax.broadcasted_iota(jnp.int32, (EXPERTS_PER_GROUP, tb), 0)
    t1 = jnp.max(lin, axis=0, keepdims=True)
    i1 = jnp.min(jnp.where(lin == t1, io8, EXPERTS_PER_GROUP), axis=0, keepdims=True)
    lin2 = jnp.where(io8 == i1, -jnp.inf, lin)
    t2 = jnp.max(lin2, axis=0, keepdims=True)
    i2 = jnp.min(jnp.where(lin2 == t2, io8, EXPERTS_PER_GROUP), axis=0, keepdims=True)
    e21 = jnp.exp(t2 - t1)
    den = 1.0 + e21
    g1 = (1.0 / den) * p_sel
    g2 = (e21 / den) * p_sel
    e1 = grp * EXPERTS_PER_GROUP + i1
    e2 = grp * EXPERTS_PER_GROUP + i2

    io32 = lax.broadcasted_iota(jnp.int32, (N_EXPERTS, tb), 0)
    m1 = io32 == e1
    m2 = io32 == e2
    mh = jnp.logical_or(m1, m2).astype(_F32)
    upper = (lax.broadcasted_iota(jnp.int32, (tb, tb), 0) < lax.broadcasted_iota(jnp.int32, (tb, tb), 1))
    cum = _mm(mh.astype(_BF), upper.astype(_BF))
    pos = cum + run_scr[...]
    rank1 = jnp.sum(jnp.where(m1, pos, 0.0), axis=0, keepdims=True).astype(jnp.int32)
    rank2 = jnp.sum(jnp.where(m2, pos, 0.0), axis=0, keepdims=True).astype(jnp.int32)
    run_new = run_scr[...] + jnp.sum(mh, axis=1, keepdims=True)
    run_scr[...] = run_new
    cnt_ref[...] = run_new

    io_r = lax.broadcasted_iota(jnp.int32, (SUBLANES, tb), 0)
    ri = jnp.where(io_r == 0, e1, jnp.where(io_r == 1, e2, jnp.where(io_r == 2, rank1,
                                                                      jnp.where(io_r == 3, rank2, 0))))
    ri_ref[0] = ri
    io_g = lax.broadcasted_iota(jnp.int32, (LANES, tb), 0)
    gmat = jnp.where(io_g == 0, g1, jnp.where(io_g == 1, g2, 0.0))
    gcol_ref[...] = gmat.T


def _route_call(h2d, g_ffn, wr_t, run0):
    t = h2d.shape[0]
    nblk = t // TOK_BLOCK
    return pl.pallas_call(
        _route_kernel,
        grid=(nblk,),
        in_specs=[
            pl.BlockSpec((TOK_BLOCK, D_MODEL), lambda i: (i, 0)),
            pl.BlockSpec((1, D_MODEL), lambda i: (0, 0)),
            pl.BlockSpec((ROUTE_ROWS, D_MODEL), lambda i: (0, 0)),
            pl.BlockSpec((N_EXPERTS, TOK_BLOCK), lambda i: (0, 0)),
        ],
        out_specs=[
            pl.BlockSpec((TOK_BLOCK, D_MODEL // 2), lambda i: (i, 0)),
            pl.BlockSpec((1, SUBLANES, TOK_BLOCK), lambda i: (i, 0, 0)),
            pl.BlockSpec((TOK_BLOCK, LANES), lambda i: (i, 0)),
            pl.BlockSpec((N_EXPERTS, TOK_BLOCK), lambda i: (0, 0)),
        ],
        out_shape=[
            jax.ShapeDtypeStruct((t, D_MODEL // 2), jnp.uint32),
            jax.ShapeDtypeStruct((nblk, SUBLANES, TOK_BLOCK), jnp.int32),
            jax.ShapeDtypeStruct((t, LANES), _F32),
            jax.ShapeDtypeStruct((N_EXPERTS, TOK_BLOCK), _F32),
        ],
        scratch_shapes=[pltpu.VMEM((N_EXPERTS, TOK_BLOCK), _F32)],
        compiler_params=pltpu.CompilerParams(dimension_semantics=("arbitrary",), vmem_limit_bytes=VMEM_LIMIT),
        name="route",
    )(h2d, g_ffn, wr_t, run0)


def _row_copy(src_ref, src_row, dst_ref, dst_row, sem):
    return pltpu.make_async_copy(src_ref.at[pl.ds(src_row, 1)], dst_ref.at[pl.ds(dst_row, 1)], sem)


def _block_wait(src_ref, dst_ref, rows, sem):
    for _ in range(TOP_K):
        pltpu.make_async_copy(src_ref.at[pl.ds(0, rows)], dst_ref.at[pl.ds(0, rows)], sem).wait()


def _dispatch_kernel(dest_ref, xn_ref, xs_in_ref, xs_ref, sems, *, tb, nblk):
    del xs_in_ref
    i = pl.program_id(0)
    base = i * (TOP_K * tb)
    row0 = i * tb
    par = lax.rem(i, 2)

    def issue(r, cv):
        for k in range(TOP_K):
            _row_copy(xn_ref, row0 + r, xs_ref, dest_ref[base + k * tb + r], sems.at[par]).start()
        return cv

    lax.fori_loop(0, tb, issue, 0, unroll=DMA_UNROLL)

    @pl.when(i > 0)
    def _():
        _block_wait(xn_ref, xs_ref, tb, sems.at[1 - par])

    @pl.when(i == nblk - 1)
    def _():
        _block_wait(xn_ref, xs_ref, tb, sems.at[par])


def _dispatch_call(dest_flat, xn_packed, xs):
    t = xn_packed.shape[0]
    nblk = t // TOK_BLOCK
    return pl.pallas_call(
        functools.partial(_dispatch_kernel, tb=TOK_BLOCK, nblk=nblk),
        grid_spec=pltpu.PrefetchScalarGridSpec(
            num_scalar_prefetch=1,
            grid=(nblk,),
            in_specs=[
                pl.BlockSpec(memory_space=pl.ANY),
                pl.BlockSpec(memory_space=pl.ANY),
            ],
            out_specs=pl.BlockSpec(memory_space=pl.ANY),
            scratch_shapes=[pltpu.SemaphoreType.DMA((2,))],
        ),
        out_shape=jax.ShapeDtypeStruct(xs.shape, xs.dtype),
        input_output_aliases={2: 0},
        compiler_params=pltpu.CompilerParams(dimension_semantics=("arbitrary",)),
        name="dispatch",
    )(dest_flat, xn_packed, xs)


def _expert_kernel(be_ref, bi_ref, nu_ref, xs_ref, wg_ref, wu_ref, wd_ref, ys_ref, wg_scr, wu_scr, wd_scr):
    i = pl.program_id(0)
    prev = be_ref[jnp.maximum(i - 1, 0)]
    fresh = jnp.logical_or(i == 0, be_ref[i] != prev)

    @pl.when(fresh)
    def _():
        wg_scr[...] = wg_ref[0].astype(_BF)
        wu_scr[...] = wu_ref[0].astype(_BF)
        wd_scr[...] = wd_ref[0].astype(_BF)

    @pl.when(i < nu_ref[0])
    def _():
        half = D_MODEL // 2
        packed = xs_ref[...]
        x_hi = lax.bitcast_convert_type(packed & jnp.uint32(0xFFFF0000), _F32).astype(_BF)
        x_lo = lax.bitcast_convert_type(packed << 16, _F32).astype(_BF)
        hg = _mm(x_hi, wg_scr[0:half, :]) + _mm(x_lo, wg_scr[half:D_MODEL, :])
        hu = _mm(x_hi, wu_scr[0:half, :]) + _mm(x_lo, wu_scr[half:D_MODEL, :])
        hb = (hg * _sigmoid(hg) * hu).astype(_BF)
        ys_ref[...] = _mm(hb, wd_scr[...])

    @pl.when(i >= nu_ref[0])
    def _():
        ys_ref[...] = jnp.zeros_like(ys_ref)


def _expert_call(blk_expert, blk_index, n_used, xs, w_gate, w_up, w_down):
    n_slots = xs.shape[0]
    n_blocks = n_slots // EXP_BLOCK
    return pl.pallas_call(
        _expert_kernel,
        grid_spec=pltpu.PrefetchScalarGridSpec(
            num_scalar_prefetch=3,
            grid=(n_blocks,),
            in_specs=[
                pl.BlockSpec((EXP_BLOCK, D_MODEL // 2), lambda i, be, bi, nu: (bi[i], 0)),
                pl.BlockSpec((1, D_MODEL, D_EXPERT), lambda i, be, bi, nu: (be[i], 0, 0)),
                pl.BlockSpec((1, D_MODEL, D_EXPERT), lambda i, be, bi, nu: (be[i], 0, 0)),
                pl.BlockSpec((1, D_EXPERT, D_MODEL), lambda i, be, bi, nu: (be[i], 0, 0)),
            ],
            out_specs=pl.BlockSpec((EXP_BLOCK, D_MODEL), lambda i, be, bi, nu: (i, 0)),
            scratch_shapes=[
                pltpu.VMEM((D_MODEL, D_EXPERT), _BF),
                pltpu.VMEM((D_MODEL, D_EXPERT), _BF),
                pltpu.VMEM((D_EXPERT, D_MODEL), _BF),
            ],
        ),
        out_shape=jax.ShapeDtypeStruct((n_slots, D_MODEL), _F32),
        compiler_params=pltpu.CompilerParams(dimension_semantics=("arbitrary",), vmem_limit_bytes=VMEM_LIMIT),
        name="experts",
    )(blk_expert, blk_index, n_used, xs, w_gate, w_up, w_down)


def _combine_kernel(dest_ref, h_ref, gcol_ref, gfin_ref, ys_ref, out_ref, y_scr, sems, *, nblk):
    i = pl.program_id(0)
    tb = h_ref.shape[0]
    par = lax.rem(i, 2)

    def gather(blk, slot):
        base = blk * (TOP_K * tb)

        def issue(r, cv):
            for k in range(TOP_K):
                _row_copy(ys_ref, dest_ref[base + k * tb + r], y_scr.at[slot, k], r,
                          sems.at[slot]).start(priority=k)
            return cv

        lax.fori_loop(0, tb, issue, 0, unroll=DMA_UNROLL)

    @pl.when(i == 0)
    def _():
        gather(0, 0)

    @pl.when(i + 1 < nblk)
    def _():
        gather(i + 1, 1 - par)

    _block_wait(ys_ref, y_scr.at[par, 0], tb, sems.at[par])

    gates = gcol_ref[...]
    moe = gates[:, 0:1] * y_scr[par, 0] + gates[:, 1:2] * y_scr[par, 1]
    y = h_ref[...] + moe
    ms = jnp.mean(y * y, axis=-1, keepdims=True)
    out_ref[...] = y * lax.rsqrt(ms + EPS) * gfin_ref[...]


def _combine_call(dest_flat, h2d, gcol, g_final, ys):
    t = h2d.shape[0]
    nblk = t // TOK_BLOCK
    return pl.pallas_call(
        functools.partial(_combine_kernel, nblk=nblk),
        grid_spec=pltpu.PrefetchScalarGridSpec(
            num_scalar_prefetch=1,
            grid=(nblk,),
            in_specs=[
                pl.BlockSpec((TOK_BLOCK, D_MODEL), lambda i, d: (i, 0)),
                pl.BlockSpec((TOK_BLOCK, LANES), lambda i, d: (i, 0)),
                pl.BlockSpec((1, D_MODEL), lambda i, d: (0, 0)),
                pl.BlockSpec(memory_space=pl.ANY),
            ],
            out_specs=pl.BlockSpec((TOK_BLOCK, D_MODEL), lambda i, d: (i, 0)),
            scratch_shapes=[
                pltpu.VMEM((2, TOP_K, TOK_BLOCK, D_MODEL), _F32),
                pltpu.SemaphoreType.DMA((2,)),
            ],
        ),
        out_shape=jax.ShapeDtypeStruct((t, D_MODEL), _F32),
        compiler_params=pltpu.CompilerParams(dimension_semantics=("arbitrary",), vmem_limit_bytes=VMEM_LIMIT),
        name="combine",
    )(dest_flat, h2d, gcol, g_final, ys)


def _dest_flat(ri, pad_starts):
    onehot = ri[:, 0:TOP_K, :, None] == jnp.arange(N_EXPERTS, dtype=jnp.int32)
    dest = jnp.sum(jnp.where(onehot, pad_starts, 0), axis=-1) + ri[:, TOP_K:2 * TOP_K, :]
    return dest.reshape(-1)


def kernel(x_prompt, x_sample, state_gla, cache_conv, meta_tokens, norm_mix_g, w_in, w_gk_up, b_gk, gla_norm_g, w_gla_out, conv_w, conv_b, conv_ln_g, conv_ln_b, w_conv_out, w_out, norm_ffn_g, w_router_group, w_router_expert, w_exp_gate, w_exp_up, w_exp_down, norm_final_g):
    depth = w_in.shape[0]
    assert depth == 1
    bp, seq, _ = x_prompt.shape
    bs, dec_seq, _ = x_sample.shape
    pad_front = (-(N_META + seq)) % CHUNK
    assert pad_front + N_META == CHUNK and seq % MIX_ROWS == 0 and dec_seq == CHUNK

    l = 0
    c_gk = 2 * QK_WIDTH + 2 * V_WIDTH
    w1 = w_in[l][:, :c_gk].astype(_BF)
    wgk = jnp.pad(w_in[l][:, c_gk:c_gk + GK_RANK], ((0, 0), (0, LANES - GK_RANK))).astype(_BF)
    w2 = w_in[l][:, c_gk + GK_RANK:].astype(_BF)
    wgu = jnp.pad(w_gk_up[l], ((0, LANES - GK_RANK), (0, 0))).astype(_BF)
    cw8 = jnp.broadcast_to(conv_w[l][:, None, :], (CONV_WIDTH, SUBLANES, CONV_DIM))
    row = lambda v: v.reshape(1, -1)
    weights = (row(norm_mix_g[l]), w1, wgk, wgu, row(b_gk[l]), row(gla_norm_g[l]), w_gla_out[l].astype(_BF),
               cw8, row(conv_b[l]), row(conv_ln_g[l]), row(conv_ln_b[l]), w_conv_out[l].astype(_BF), w2,
               w_out[l].astype(_BF))

    x_meta = jnp.concatenate([jnp.zeros((pad_front, D_MODEL), _F32), meta_tokens.astype(_F32)], axis=0)[None]
    zero_state = jnp.zeros((1, GLA_HEADS, GLA_DK, GLA_DV), _F32)
    zero_hist = jnp.zeros((1, HIST_ROWS, CONV_DIM), _F32)
    _, s_meta, h_meta = _mixer_call(x_meta, zero_state, zero_hist, weights, nseg=1, seg=CHUNK, carry=True,
                                    n_front_pad=pad_front, shared_init=True)
    h_p, s_p, hist_p = _mixer_call(x_prompt, s_meta, h_meta, weights, nseg=1, seg=MIX_ROWS, carry=True,
                                   n_front_pad=0, shared_init=True)
    hist_s_in = jnp.pad(cache_conv[l], ((0, 0), (HIST_SKIP, 0), (0, 0)))
    h_s, s_s, hist_s = _mixer_call(x_sample, state_gla[l], hist_s_in, weights, nseg=SAMPLE_SEGS, seg=CHUNK,
                                   carry=False, n_front_pad=0, shared_init=False)

    wr_t = jnp.concatenate([w_router_group[l].T, jnp.zeros((SUBLANES - N_GROUPS, D_MODEL), _F32),
                            w_router_expert[l].T], axis=0)
    g_ffn = row(norm_ffn_g[l])
    hp2 = h_p.reshape(bp * seq, D_MODEL)
    hs2 = h_s.reshape(bs * dec_seq, D_MODEL)
    run0 = jnp.zeros((N_EXPERTS, TOK_BLOCK), _F32)
    xn_p, ri_p, gc_p, cnt_p = _route_call(hp2, g_ffn, wr_t, run0)
    xn_s, ri_s, gc_s, cnt_s = _route_call(hs2, g_ffn, wr_t, cnt_p)

    counts = cnt_s[:, 0].astype(jnp.int32)
    padded = (counts + EXP_BLOCK - 1) // EXP_BLOCK * EXP_BLOCK
    pad_ends = jnp.cumsum(padded)
    pad_starts = pad_ends - padded
    n_assign = TOP_K * (bp * seq + bs * dec_seq)
    n_blocks = -(-n_assign // EXP_BLOCK) + N_EXPERTS
    n_used = pad_ends[-1] // EXP_BLOCK
    blk_ids = jnp.minimum(jnp.arange(n_blocks, dtype=jnp.int32), n_used - 1)
    blk_expert = jnp.minimum(jnp.sum(pad_ends[None, :] <= (blk_ids * EXP_BLOCK)[:, None], axis=1),
                             N_EXPERTS - 1).astype(jnp.int32)
    dest_p = _dest_flat(ri_p, pad_starts)
    dest_s = _dest_flat(ri_s, pad_starts)

    xs = jnp.zeros((n_blocks * EXP_BLOCK, D_MODEL // 2), jnp.uint32)
    xs = _dispatch_call(dest_p, xn_p, xs)
    xs = _dispatch_call(dest_s, xn_s, xs)
    ys = _expert_call(blk_expert, blk_ids, n_used.reshape(1).astype(jnp.int32), xs,
                      w_exp_gate[l], w_exp_up[l], w_exp_down[l])
    g_fin = row(norm_final_g)
    y_p = _combine_call(dest_p, hp2, gc_p, g_fin, ys)
    y_s = _combine_call(dest_s, hs2, gc_s, g_fin, ys)

    return (y_p.reshape(bp, seq, D_MODEL), y_s.reshape(bs, dec_seq, D_MODEL),
            s_p[None], hist_p[None, :, HIST_SKIP:, :], s_s[None], hist_s[None, :, HIST_SKIP:, :])
```
